```python
import math
import jax, jax.numpy as jnp
from jax import lax
import numpy as np

D_MODEL = 1024
BATCH = 16
SEQ = 2048
DEPTH = 1

RET_HEADS = 4
RET_DK = 128
RET_DV = 256
RET_CHUNK = 128
RET_ROT_BASE = 10000.0
MOBA_HEADS = 8
MOBA_DH = 64
MOBA_BLOCK = 256
MOBA_TOPK = 3
MOBA_QCHUNK = 128
ROPE_THETA = 500000.0
ROPE_DIMS = MOBA_DH // 4
D_FF = 2816
LN_EPS = 1e-5
GN_EPS = 1e-5
DEEPNORM_ALPHA = (2.0 * DEPTH) ** 0.25
DEEPNORM_BETA = (8.0 * DEPTH) ** -0.25
W_IN_SPLITS = (
    RET_HEADS * RET_DK,
    RET_HEADS * RET_DK,
    RET_HEADS * RET_DV,
    RET_HEADS * RET_DV,
    MOBA_HEADS * MOBA_DH,
    MOBA_HEADS * MOBA_DH,
    MOBA_HEADS * MOBA_DH,
    2 * D_MODEL,
)
W_IN_COLS = sum(W_IN_SPLITS)

kernel_name = "hybrid_retention_moba_macaron_deepnorm"


def layer_norm(x, g, b):
    xf = x.astype(jnp.float32)
    mu = jnp.mean(xf, axis=-1, keepdims=True)
    var = jnp.mean(jnp.square(xf - mu), axis=-1, keepdims=True)
    y = (xf - mu) * lax.rsqrt(var + LN_EPS) * g.astype(jnp.float32) + b.astype(jnp.float32)
    return y.astype(x.dtype)


def swiglu(x, w_gu, w_down):
    gate, up = jnp.split(x @ w_gu, 2, axis=-1)
    return (jax.nn.silu(gate) * up) @ w_down


def rotate(x, pos, inv_freq):
    ang = pos[:, None].astype(jnp.float32) * inv_freq[None, :]
    cos = jnp.cos(ang)[None, :, None, :]
    sin = jnp.sin(ang)[None, :, None, :]
    x1, x2 = jnp.split(x, 2, axis=-1)
    return jnp.concatenate([x1 * cos - x2 * sin, x1 * sin + x2 * cos], axis=-1)


def retention(q, k, v):
    B, S, H, DK = q.shape
    DV = v.shape[-1]
    C = RET_CHUNK
    n = S // C
    pos = jnp.arange(S)
    inv = 1.0 / (RET_ROT_BASE ** jnp.linspace(0.0, 1.0, DK // 2, dtype=jnp.float32))
    q = rotate(q, pos, inv)
    k = rotate(k, pos, inv) * (DK ** -0.5)
    log_g = jnp.log(1.0 - 2.0 ** (-5.0 - jnp.arange(H, dtype=jnp.float32)))

    qc = q.reshape(B, n, C, H, DK).transpose(0, 3, 1, 2, 4)
    kc = k.reshape(B, n, C, H, DK).transpose(0, 3, 1, 2, 4)
    vc = v.reshape(B, n, C, H, DV).transpose(0, 3, 1, 2, 4)

    idx = jnp.arange(C, dtype=jnp.float32)
    diff = idx[:, None] - idx[None, :]
    d_intra = jnp.where(diff >= 0, jnp.exp(log_g[:, None, None] * jnp.maximum(diff, 0.0)), 0.0)
    scores = jnp.einsum('bhncd,bhnkd->bhnck', qc, kc) * d_intra[:, None]
    intra = jnp.einsum('bhnck,bhnke->bhnce', scores, vc)

    k_dec = kc * jnp.exp(log_g[:, None] * (C - 1.0 - idx)[None, :])[:, None, :, None]
    kv = jnp.einsum('bhnkd,bhnke->nbhde', k_dec, vc)
    chunk_decay = jnp.exp(log_g * C)[None, :, None, None]

    def step(state, kv_n):
        return chunk_decay * state + kv_n, state

    _, prev = lax.scan(step, jnp.zeros((B, H, DK, DV), jnp.float32), kv)
    q_dec = qc * jnp.exp(log_g[:, None] * (idx + 1.0)[None, :])[:, None, :, None]
    cross = jnp.einsum('bhncd,nbhde->bhnce', q_dec, prev)

    out = (intra + cross).transpose(0, 2, 3, 1, 4).reshape(B, S, H, DV)
    mu = jnp.mean(out, axis=-1, keepdims=True)
    var = jnp.mean(jnp.square(out - mu), axis=-1, keepdims=True)
    out = (out - mu) * lax.rsqrt(var + GN_EPS)
    return out.reshape(B, S, H * DV)


def moba_attention(q, k, v):
    B, S, H, D = q.shape
    BLK = MOBA_BLOCK
    QC = MOBA_QCHUNK
    pos = jnp.arange(S)
    inv = 1.0 / (ROPE_THETA ** (jnp.arange(0, ROPE_DIMS, 2, dtype=jnp.float32) / ROPE_DIMS))
    q = jnp.concatenate([rotate(q[..., :ROPE_DIMS], pos, inv), q[..., ROPE_DIMS:]], axis=-1)
    k = jnp.concatenate([rotate(k[..., :ROPE_DIMS], pos, inv), k[..., ROPE_DIMS:]], axis=-1)

    nblk = -(-S // BLK)
    s_pad = nblk * BLK
    padw = ((0, 0), (0, s_pad - S), (0, 0), (0, 0))
    kb = jnp.pad(k, padw).reshape(B, nblk, BLK, H, D).transpose(0, 3, 1, 2, 4)
    vb = jnp.pad(v, padw).reshape(B, nblk, BLK, H, D).transpose(0, 3, 1, 2, 4)
    k_mean = jnp.mean(kb, axis=3)

    qh = q.transpose(0, 2, 1, 3)
    gate = jnp.einsum('bhsd,bhnd->bhsn', qh, k_mean)
    cur = pos // BLK
    past = jnp.arange(nblk)[None, :] < cur[:, None]
    gate = jnp.where(past[None, None], gate, -jnp.inf)
    n_sel = min(MOBA_TOPK, nblk)
    _, top_idx = lax.top_k(gate, n_sel)

    nqc = S // QC
    q_chunks = qh.reshape(B, H, nqc, QC, D).transpose(0, 2, 1, 3, 4).reshape(B * nqc, H, QC, D)
    i_chunks = top_idx.reshape(B, H, nqc, QC, n_sel).transpose(0, 2, 1, 3, 4).reshape(B * nqc, H, QC, n_sel)
    scale = D ** -0.5

    def attend(args):
        i, q_c, idx_c = args
        b = i // nqc
        c = i % nqc
        kb_b = lax.dynamic_index_in_dim(kb, b, 0, keepdims=False)
        vb_b = lax.dynamic_index_in_dim(vb, b, 0, keepdims=False)
        gk = jax.vmap(lambda kh, ih: kh[ih])(kb_b, idx_c)
        gv = jax.vmap(lambda vh, ih: vh[ih])(vb_b, idx_c)
        qpos = c * QC + jnp.arange(QC)
        s_sel = jnp.einsum('hqd,hqjpd->hqjp', q_c, gk) * scale
        sel_ok = jnp.arange(n_sel)[None, :] < (qpos // BLK)[:, None]
        s_sel = jnp.where(sel_ok[None, :, :, None], s_sel, -jnp.inf)
        own = (c * QC) // BLK
        ko = lax.dynamic_index_in_dim(kb_b, own, 1, keepdims=False)
        vo = lax.dynamic_index_in_dim(vb_b, own, 1, keepdims=False)
        s_own = jnp.einsum('hqd,hpd->hqp', q_c, ko) * scale
        kpos = own * BLK + jnp.arange(BLK)
        s_own = jnp.where((kpos[None, :] <= qpos[:, None])[None], s_own, -jnp.inf)
        logits = jnp.concatenate([s_sel.reshape(H, QC, n_sel * BLK), s_own], axis=-1)
        p = jax.nn.softmax(logits, axis=-1)
        p_sel = p[..., :n_sel * BLK].reshape(H, QC, n_sel, BLK)
        p_own = p[..., n_sel * BLK:]
        return jnp.einsum('hqjp,hqjpd->hqd', p_sel, gv) + jnp.einsum('hqp,hpd->hqd', p_own, vo)

    out = lax.map(attend, (jnp.arange(B * nqc), q_chunks, i_chunks))
    return out.reshape(B, nqc, H, QC, D).transpose(0, 1, 3, 2, 4).reshape(B, S, H * D)


def hybrid_mixer(h, w_in, ret_proj, moba_proj, w_out):
    B, S, _ = h.shape
    f32 = jnp.float32
    offs = np.cumsum(W_IN_SPLITS)[:-1].tolist()
    rq, rk, rv, rg, mq, mk, mv, gates = jnp.split(h @ w_in, offs, axis=-1)
    y_ret = retention(rq.reshape(B, S, RET_HEADS, RET_DK).astype(f32),
                      rk.reshape(B, S, RET_HEADS, RET_DK).astype(f32),
                      rv.reshape(B, S, RET_HEADS, RET_DV).astype(f32))
    y_ret = y_ret * jax.nn.silu(rg.astype(f32))
    y_a = y_ret.astype(h.dtype) @ ret_proj
    y_moba = moba_attention(mq.reshape(B, S, MOBA_HEADS, MOBA_DH).astype(f32),
                            mk.reshape(B, S, MOBA_HEADS, MOBA_DH).astype(f32),
                            mv.reshape(B, S, MOBA_HEADS, MOBA_DH).astype(f32))
    y_b = y_moba.astype(h.dtype) @ moba_proj
    g_a, g_b = jnp.split(jax.nn.sigmoid(gates), 2, axis=-1)
    return (g_a * y_a + g_b * y_b) @ w_out


def setup_inputs(seed: int = 0) -> dict:
    key = jax.random.key(seed)
    ks = jax.random.split(key, 16)
    f32 = jnp.float32
    L = DEPTH

    def nrm(k, shape, scale):
        return jax.random.normal(k, shape, f32) * scale

    return {
        "x": jax.random.normal(ks[0], (BATCH, SEQ, D_MODEL), f32),
        "ln1_g": 1.0 + nrm(ks[1], (L, D_MODEL), 0.02),
        "ln1_b": nrm(ks[2], (L, D_MODEL), 0.02),
        "ffn1_w_gu": nrm(ks[3], (L, D_MODEL, 2 * D_FF), D_MODEL ** -0.5),
        "ffn1_w_down": nrm(ks[4], (L, D_FF, D_MODEL), DEEPNORM_BETA * D_FF ** -0.5),
        "w_in": nrm(ks[5], (L, D_MODEL, W_IN_COLS), D_MODEL ** -0.5),
        "ret_proj": nrm(ks[6], (L, RET_HEADS * RET_DV, D_MODEL), (RET_HEADS * RET_DV) ** -0.5),
        "moba_proj": nrm(ks[7], (L, MOBA_HEADS * MOBA_DH, D_MODEL), (MOBA_HEADS * MOBA_DH) ** -0.5),
        "w_out": nrm(ks[8], (L, D_MODEL, D_MODEL), DEEPNORM_BETA * D_MODEL ** -0.5),
        "lnm_g": 1.0 + nrm(ks[9], (L, D_MODEL), 0.02),
        "lnm_b": nrm(ks[10], (L, D_MODEL), 0.02),
        "ffn2_w_gu": nrm(ks[11], (L, D_MODEL, 2 * D_FF), D_MODEL ** -0.5),
        "ffn2_w_down": nrm(ks[12], (L, D_FF, D_MODEL), DEEPNORM_BETA * D_FF ** -0.5),
        "ln2_g": 1.0 + nrm(ks[13], (L, D_MODEL), 0.02),
        "ln2_b": nrm(ks[14], (L, D_MODEL), 0.02),
    }


def reference(x, ln1_g, ln1_b, ffn1_w_gu, ffn1_w_down, w_in, ret_proj, moba_proj, w_out,
              lnm_g, lnm_b, ffn2_w_gu, ffn2_w_down, ln2_g, ln2_b):
    for l in range(DEPTH):
        x = layer_norm(DEEPNORM_ALPHA * x + 0.5 * swiglu(x, ffn1_w_gu[l], ffn1_w_down[l]), ln1_g[l], ln1_b[l])
        x = layer_norm(DEEPNORM_ALPHA * x + hybrid_mixer(x, w_in[l], ret_proj[l], moba_proj[l], w_out[l]),
                       lnm_g[l], lnm_b[l])
        x = layer_norm(DEEPNORM_ALPHA * x + 0.5 * swiglu(x, ffn2_w_gu[l], ffn2_w_down[l]), ln2_g[l], ln2_b[l])
    return x
```

```python
import functools
import math

import jax
import jax.numpy as jnp
from jax import lax
from jax.experimental import pallas as pl
from jax.experimental.pallas import tpu as pltpu

F32 = jnp.float32
BF16 = jnp.bfloat16

RET_HEADS = 4
RET_DK = 128
RET_DV = 256
RET_CHUNK = 128
RET_ROT_BASE = 10000.0
MOBA_HEADS = 8
MOBA_DH = 64
MOBA_BLOCK = 256
MOBA_TOPK = 3
ROPE_THETA = 500000.0
ROPE_DIMS = MOBA_DH // 4
LN_EPS = 1e-5
GN_EPS = 1e-5

V7X_VMEM_LIMIT_BYTES = 56 * 1024 * 1024
LANES = 128

NT_DIMS = (((1,), (1,)), ((), ()))
TN_DIMS = (((0,), (0,)), ((), ()))


def _cparams(*sem):
    return pltpu.CompilerParams(dimension_semantics=sem, vmem_limit_bytes=V7X_VMEM_LIMIT_BYTES)


def _resident(shape):
    nd = len(shape)
    return pl.BlockSpec(shape, lambda *_: (0,) * nd, pipeline_mode=pl.Buffered(1))


def _layer_norm(y, g, b):
    mu = jnp.mean(y, axis=-1, keepdims=True)
    yc = y - mu
    var = jnp.mean(yc * yc, axis=-1, keepdims=True)
    return yc * lax.rsqrt(var + LN_EPS) * g + b


def _silu(x):
    return x * jax.nn.sigmoid(x)


def _ffn_ln_kernel(x_ref, wg_ref, wu_ref, wd_ref, g_ref, b_ref, o_ref, *, alpha):
    x = x_ref[...]
    xb = x.astype(BF16)
    gate = jnp.dot(xb, wg_ref[...], preferred_element_type=F32)
    up = jnp.dot(xb, wu_ref[...], preferred_element_type=F32)
    mid = (_silu(gate) * up).astype(BF16)
    ffn = jnp.dot(mid, wd_ref[...], preferred_element_type=F32)
    o_ref[...] = _layer_norm(alpha * x + 0.5 * ffn, g_ref[...], b_ref[...])


def _ffn_ln(x, w_gu, w_down, g, b, *, alpha, tm):
    T, D = x.shape
    FF = w_down.shape[0]
    wg = w_gu[:, :FF].astype(BF16)
    wu = w_gu[:, FF:].astype(BF16)
    wd = w_down.astype(BF16)
    return pl.pallas_call(
        functools.partial(_ffn_ln_kernel, alpha=alpha),
        grid=(T // tm,),
        in_specs=[
            pl.BlockSpec((tm, D), lambda i: (i, 0)),
            _resident((D, FF)),
            _resident((D, FF)),
            _resident((FF, D)),
            _resident((1, D)),
            _resident((1, D)),
        ],
        out_specs=pl.BlockSpec((tm, D), lambda i: (i, 0)),
        out_shape=jax.ShapeDtypeStruct((T, D), F32),
        compiler_params=_cparams("parallel"),
        name="ffn_ln",
    )(x, wg, wu, wd, g.reshape(1, D), b.reshape(1, D))


def _in_proj_kernel(h_ref, wrq_ref, wrk_ref, wrv_ref, wrg_ref, wmq_ref, wmk_ref, wmv_ref, wga_ref, wgb_ref,
                    rcos_ref, rsin_ref, mcos_ref, msa_ref, msb_ref,
                    rq_ref, rk_ref, rv_ref, rg_ref, mq_ref, mk_ref, mv_ref, ga_ref, gb_ref):
    hb = h_ref[...].astype(BF16)

    def proj(w_ref):
        return jnp.dot(hb, w_ref[...], preferred_element_type=F32)

    rcos, rsin = rcos_ref[...], rsin_ref[...]
    mcos, msa, msb = mcos_ref[...], msa_ref[...], msb_ref[...]

    def ret_rot(x):
        parts = []
        for hd in range(RET_HEADS):
            xh = x[:, hd * RET_DK:(hd + 1) * RET_DK]
            parts.append(xh * rcos + pltpu.roll(xh, RET_DK // 2, 1) * rsin)
        return jnp.concatenate(parts, axis=1)

    def moba_rot(x):
        parts = []
        half = ROPE_DIMS // 2
        for c in range(x.shape[1] // LANES):
            xc = x[:, c * LANES:(c + 1) * LANES]
            parts.append(xc * mcos + pltpu.roll(xc, LANES - half, 1) * msa + pltpu.roll(xc, half, 1) * msb)
        return jnp.concatenate(parts, axis=1)

    rq_ref[...] = ret_rot(proj(wrq_ref))
    rk_ref[...] = ret_rot(proj(wrk_ref)) * (RET_DK ** -0.5)
    rv_ref[...] = proj(wrv_ref).astype(rv_ref.dtype)
    rg_ref[...] = proj(wrg_ref)
    mq_ref[...] = moba_rot(proj(wmq_ref))
    mk_ref[...] = moba_rot(proj(wmk_ref))
    mv_ref[...] = proj(wmv_ref).astype(mv_ref.dtype)
    ga_ref[...] = jax.nn.sigmoid(proj(wga_ref))
    gb_ref[...] = jax.nn.sigmoid(proj(wgb_ref))


def _rotary_tables(S):
    pos = jnp.arange(S).astype(F32)
    inv = 1.0 / (RET_ROT_BASE ** jnp.linspace(0.0, 1.0, RET_DK // 2, dtype=F32))
    ang = pos[:, None] * inv[None, :]
    cos, sin = jnp.cos(ang), jnp.sin(ang)
    rcos = jnp.concatenate([cos, cos], axis=1)
    rsin = jnp.concatenate([-sin, sin], axis=1)
    inv = 1.0 / (ROPE_THETA ** (jnp.arange(0, ROPE_DIMS, 2, dtype=F32) / ROPE_DIMS))
    ang = pos[:, None] * inv[None, :]
    cos, sin = jnp.cos(ang), jnp.sin(ang)
    half = ROPE_DIMS // 2
    rest = MOBA_DH - ROPE_DIMS
    ones = jnp.ones((S, rest), F32)
    zeros = jnp.zeros((S, rest), F32)
    zh = jnp.zeros((S, half), F32)
    reps = LANES // MOBA_DH
    mcos = jnp.tile(jnp.concatenate([cos, cos, ones], axis=1), (1, reps))
    msa = jnp.tile(jnp.concatenate([-sin, zh, zeros], axis=1), (1, reps))
    msb = jnp.tile(jnp.concatenate([zh, sin, zeros], axis=1), (1, reps))
    return rcos, rsin, mcos, msa, msb


def _in_proj(h, w_in, S, *, tm):
    T, D = h.shape
    nqk = RET_HEADS * RET_DK
    nv = RET_HEADS * RET_DV
    nm = MOBA_HEADS * MOBA_DH
    sizes = (nqk, nqk, nv, nv, nm, nm, nm, D, D)
    offs = [0]
    for s in sizes:
        offs.append(offs[-1] + s)
    assert offs[-1] == w_in.shape[1]
    wb = w_in.astype(BF16)
    ws = [wb[:, offs[i]:offs[i + 1]] for i in range(len(sizes))]
    tabs = _rotary_tables(S)
    spt = S // tm
    tab_spec = pl.BlockSpec((tm, LANES), lambda i: (i % spt, 0))
    out_dtypes = (F32, F32, BF16, F32, F32, F32, BF16, F32, F32)
    return pl.pallas_call(
        _in_proj_kernel,
        grid=(T // tm,),
        in_specs=[pl.BlockSpec((tm, D), lambda i: (i, 0))]
        + [_resident((D, s)) for s in sizes]
        + [tab_spec] * 5,
        out_specs=[pl.BlockSpec((tm, s), lambda i: (i, 0)) for s in sizes],
        out_shape=[jax.ShapeDtypeStruct((T, s), dt) for s, dt in zip(sizes, out_dtypes)],
        compiler_params=_cparams("parallel"),
        name="in_proj",
    )(h, *ws, *tabs)


def _retention_kernel(q_ref, k_ref, v_ref, g_ref, dmat_ref, qdec_ref, kdec_ref, cdec_ref, o_ref, state_ref, *, n_chunks):
    C = RET_CHUNK
    state_ref[...] = jnp.zeros_like(state_ref)
    dmat = dmat_ref[0]
    qdec = qdec_ref[0]
    kdec = kdec_ref[0]
    cdec = cdec_ref[0]

    def chunk(c, carry):
        rows = pl.ds(pl.multiple_of(c * C, C), C)
        q = q_ref[0, rows, :]
        k = k_ref[0, rows, :]
        v = v_ref[0, rows, :]
        state = state_ref[...]
        scores = lax.dot_general(q.astype(BF16), k.astype(BF16), NT_DIMS, preferred_element_type=F32) * dmat
        intra = jnp.dot(scores.astype(BF16), v, preferred_element_type=F32)
        cross = jnp.dot((q * qdec).astype(BF16), state.astype(BF16), preferred_element_type=F32)
        kv = lax.dot_general((k * kdec).astype(BF16), v, TN_DIMS, preferred_element_type=F32)
        state_ref[...] = cdec * state + kv
        out = intra + cross
        mu = jnp.mean(out, axis=-1, keepdims=True)
        oc = out - mu
        var = jnp.mean(oc * oc, axis=-1, keepdims=True)
        y = oc * lax.rsqrt(var + GN_EPS) * _silu(g_ref[0, rows, :])
        o_ref[0, rows, :] = y.astype(o_ref.dtype)
        return carry

    lax.fori_loop(0, n_chunks, chunk, 0)


def _retention_tables():
    C = RET_CHUNK
    log_g = jnp.log(1.0 - 2.0 ** (-5.0 - jnp.arange(RET_HEADS, dtype=F32)))
    idx = jnp.arange(C, dtype=F32)
    diff = idx[:, None] - idx[None, :]
    dmat = jnp.where(diff >= 0, jnp.exp(log_g[:, None, None] * jnp.maximum(diff, 0.0)), 0.0)
    kdec = jnp.exp(log_g[:, None] * (C - 1.0 - idx)[None, :])
    qdec = jnp.exp(log_g[:, None] * (idx + 1.0)[None, :])
    cdec = jnp.exp(log_g * C)
    kdec = jnp.broadcast_to(kdec[:, :, None], (RET_HEADS, C, RET_DK))
    qdec = jnp.broadcast_to(qdec[:, :, None], (RET_HEADS, C, RET_DK))
    cdec = jnp.broadcast_to(cdec[:, None, None], (RET_HEADS, 1, RET_DV))
    return dmat, qdec, kdec, cdec


def _retention(rq, rk, rv, rg):
    B, S, _ = rq.shape
    C = RET_CHUNK
    dmat, qdec, kdec, cdec = _retention_tables()
    qk_spec = pl.BlockSpec((1, S, RET_DK), lambda b, h: (b, 0, h))
    v_spec = pl.BlockSpec((1, S, RET_DV), lambda b, h: (b, 0, h))
    return pl.pallas_call(
        functools.partial(_retention_kernel, n_chunks=S // C),
        grid=(B, RET_HEADS),
        in_specs=[
            qk_spec, qk_spec, v_spec, v_spec,
            pl.BlockSpec((1, C, C), lambda b, h: (h, 0, 0)),
            pl.BlockSpec((1, C, RET_DK), lambda b, h: (h, 0, 0)),
            pl.BlockSpec((1, C, RET_DK), lambda b, h: (h, 0, 0)),
            pl.BlockSpec((1, 1, RET_DV), lambda b, h: (h, 0, 0)),
        ],
        out_specs=v_spec,
        out_shape=jax.ShapeDtypeStruct((B, S, RET_HEADS * RET_DV), BF16),
        scratch_shapes=[pltpu.VMEM((RET_DK, RET_DV), F32)],
        compiler_params=_cparams("parallel", "parallel"),
        name="retention",
    )(rq, rk, rv, rg, dmat, qdec, kdec, cdec)


def _moba_kernel(q_ref, k_ref, v_ref, o_ref, *, n_blocks):
    BLK = MOBA_BLOCK
    D = MOBA_DH
    scale = D ** -0.5
    for hh in range(LANES // D):
        cols = slice(hh * D, (hh + 1) * D)
        q = q_ref[0, :, cols]
        k = k_ref[0, :, cols]
        v = v_ref[0, :, cols]
        k_mean = jnp.concatenate(
            [jnp.mean(k[j * BLK:(j + 1) * BLK], axis=0, keepdims=True) for j in range(n_blocks)], axis=0)
        gate = lax.dot_general(q, k_mean, NT_DIMS, preferred_element_type=F32,
                               precision=lax.Precision.HIGHEST)
        qb = (q * scale).astype(BF16)
        kb = k.astype(BF16)
        blk_id = lax.broadcasted_iota(jnp.int32, (BLK, n_blocks), 1)
        row = lax.broadcasted_iota(jnp.int32, (BLK, BLK), 0)
        col = lax.broadcasted_iota(jnp.int32, (BLK, BLK), 1)
        causal_bias = jnp.where(col <= row, 0.0, -jnp.inf).astype(F32)
        for i in range(n_blocks):
            qi = qb[i * BLK:(i + 1) * BLK]
            n_keys = (i + 1) * BLK
            s = lax.dot_general(qi, kb[:n_keys], NT_DIMS, preferred_element_type=F32)
            if i > 0:
                g = gate[i * BLK:(i + 1) * BLK]
                rank = jnp.zeros((BLK, n_blocks), jnp.int32)
                for jp in range(i):
                    gj = g[:, jp:jp + 1]
                    beats = (gj > g) | ((gj == g) & (jp < blk_id))
                    rank = rank + beats.astype(jnp.int32)
                sel_bias = jnp.where((rank < MOBA_TOPK) & (blk_id < i), 0.0, -jnp.inf).astype(F32)
                bias = jnp.concatenate(
                    [jnp.broadcast_to(sel_bias[:, j:j + 1], (BLK, BLK)) for j in range(i)] + [causal_bias], axis=1)
            else:
                bias = causal_bias
            s = s + bias
            m = jnp.max(s, axis=-1, keepdims=True)
            p = jnp.exp(s - m)
            l = jnp.sum(p, axis=-1, keepdims=True)
            acc = jnp.dot(p.astype(BF16), v[:n_keys], preferred_element_type=F32)
            o_ref[0, i * BLK:(i + 1) * BLK, cols] = (acc / l).astype(o_ref.dtype)


def _moba(mq, mk, mv):
    B, S, W = mq.shape
    spec = pl.BlockSpec((1, S, LANES), lambda b, c: (b, 0, c))
    return pl.pallas_call(
        functools.partial(_moba_kernel, n_blocks=S // MOBA_BLOCK),
        grid=(B, W // LANES),
        in_specs=[spec, spec, spec],
        out_specs=spec,
        out_shape=jax.ShapeDtypeStruct((B, S, W), BF16),
        compiler_params=_cparams("parallel", "parallel"),
        name="moba",
    )(mq, mk, mv)


def _mix_ln_kernel(h_ref, yr_ref, ym_ref, ga_ref, gb_ref, pa_ref, pb_ref, wo_ref, g_ref, b_ref, o_ref, *, alpha):
    ya = jnp.dot(yr_ref[...], pa_ref[...], preferred_element_type=F32)
    yb = jnp.dot(ym_ref[...], pb_ref[...], preferred_element_type=F32)
    mix = (ga_ref[...] * ya + gb_ref[...] * yb).astype(BF16)
    out = jnp.dot(mix, wo_ref[...], preferred_element_type=F32)
    o_ref[...] = _layer_norm(alpha * h_ref[...] + out, g_ref[...], b_ref[...])


def _mix_ln(h, y_ret, y_moba, ga, gb, ret_proj, moba_proj, w_out, g, b, *, alpha, tm):
    T, D = h.shape
    NR = y_ret.shape[1]
    NM = y_moba.shape[1]

    def tile(n):
        return pl.BlockSpec((tm, n), lambda i: (i, 0))

    return pl.pallas_call(
        functools.partial(_mix_ln_kernel, alpha=alpha),
        grid=(T // tm,),
        in_specs=[tile(D), tile(NR), tile(NM), tile(D), tile(D),
                  _resident((NR, D)), _resident((NM, D)), _resident((D, D)), _resident((1, D)), _resident((1, D))],
        out_specs=tile(D),
        out_shape=jax.ShapeDtypeStruct((T, D), F32),
        compiler_params=_cparams("parallel"),
        name="mix_ln",
    )(h, y_ret, y_moba, ga, gb, ret_proj.astype(BF16), moba_proj.astype(BF16), w_out.astype(BF16),
      g.reshape(1, D), b.reshape(1, D))


def kernel(x, ln1_g, ln1_b, ffn1_w_gu, ffn1_w_down, w_in, ret_proj, moba_proj, w_out, lnm_g, lnm_b,
           ffn2_w_gu, ffn2_w_down, ln2_g, ln2_b):
    B, S, D = x.shape
    depth = ffn1_w_gu.shape[0]
    alpha = (2.0 * depth) ** 0.25
    T = B * S
    h = x.reshape(T, D)
    for l in range(depth):
        h = _ffn_ln(h, ffn1_w_gu[l], ffn1_w_down[l], ln1_g[l], ln1_b[l], alpha=alpha, tm=512)
        rq, rk, rv, rg, mq, mk, mv, ga, gb = _in_proj(h, w_in[l], S, tm=256)
        y_ret = _retention(rq.reshape(B, S, -1), rk.reshape(B, S, -1), rv.reshape(B, S, -1), rg.reshape(B, S, -1))
        y_moba = _moba(mq.reshape(B, S, -1), mk.reshape(B, S, -1), mv.reshape(B, S, -1))
        h = _mix_ln(h, y_ret.reshape(T, -1), y_moba.reshape(T, -1), ga, gb, ret_proj[l], moba_proj[l], w_out[l],
                    lnm_g[l], lnm_b[l], alpha=alpha, tm=512)
        h = _ffn_ln(h, ffn2_w_gu[l], ffn2_w_down[l], ln2_g[l], ln2_b[l], alpha=alpha, tm=512)
    return h.reshape(B, S, D)
```

```python
import functools
import math

import jax
import jax.numpy as jnp
from jax import lax
from jax.experimental import pallas as pl
from jax.experimental.pallas import tpu as pltpu

F32 = jnp.float32
BF16 = jnp.bfloat16

RET_HEADS = 4
RET_DK = 128
RET_DV = 256
RET_CHUNK = 128
RET_ROT_BASE = 10000.0
MOBA_HEADS = 8
MOBA_DH = 64
MOBA_BLOCK = 256
MOBA_TOPK = 3
ROPE_THETA = 500000.0
ROPE_DIMS = MOBA_DH // 4
LN_EPS = 1e-5
GN_EPS = 1e-5

V7X_VMEM_LIMIT_BYTES = 56 * 1024 * 1024
LANES = 128
SUBLANES = 8
MASKED = -1e30

NT_DIMS = (((1,), (1,)), ((), ()))
TN_DIMS = (((0,), (0,)), ((), ()))


def _cparams(*sem):
    return pltpu.CompilerParams(dimension_semantics=sem, vmem_limit_bytes=V7X_VMEM_LIMIT_BYTES)


def _resident(shape):
    nd = len(shape)
    return pl.BlockSpec(shape, lambda *_: (0,) * nd, pipeline_mode=pl.Buffered(1))


def _layer_norm(y, g, b):
    mu = jnp.mean(y, axis=-1, keepdims=True)
    yc = y - mu
    var = jnp.mean(yc * yc, axis=-1, keepdims=True)
    return yc * lax.rsqrt(var + LN_EPS) * g + b


def _silu(x):
    return x * jax.nn.sigmoid(x)


def _ret_decay(h):
    return 1.0 - 2.0 ** (-5.0 - h)


def _ffn_ln_kernel(x_ref, wg_ref, wu_ref, wd_ref, g_ref, b_ref, o_ref, *, alpha):
    x = x_ref[...]
    xb = x.astype(BF16)
    gate = jnp.dot(xb, wg_ref[...], preferred_element_type=F32)
    up = jnp.dot(xb, wu_ref[...], preferred_element_type=F32)
    mid = (_silu(gate) * up).astype(BF16)
    ffn = jnp.dot(mid, wd_ref[...], preferred_element_type=F32)
    o_ref[...] = _layer_norm(alpha * x + 0.5 * ffn, g_ref[...], b_ref[...])


def _ffn_ln(x, w_gu, w_down, g, b, *, alpha, tm):
    T, D = x.shape
    FF = w_down.shape[0]
    wg = w_gu[:, :FF].astype(BF16)
    wu = w_gu[:, FF:].astype(BF16)
    wd = w_down.astype(BF16)
    return pl.pallas_call(
        functools.partial(_ffn_ln_kernel, alpha=alpha),
        grid=(T // tm,),
        in_specs=[
            pl.BlockSpec((tm, D), lambda i: (i, 0)),
            _resident((D, FF)),
            _resident((D, FF)),
            _resident((FF, D)),
            _resident((1, D)),
            _resident((1, D)),
        ],
        out_specs=pl.BlockSpec((tm, D), lambda i: (i, 0)),
        out_shape=jax.ShapeDtypeStruct((T, D), F32),
        compiler_params=_cparams("parallel"),
        name="ffn_ln",
    )(x, wg, wu, wd, g.reshape(1, D), b.reshape(1, D))


def _in_proj_kernel(h_ref, wrq_ref, wrk_ref, wrv_ref, wrg_ref, wmq_ref, wmk_ref, wmv_ref, wga_ref, wgb_ref,
                    qcos_ref, qsin_ref, kcos_ref, ksin_ref, mcos_ref, msa_ref, msb_ref,
                    rq_ref, rk_ref, rv_ref, sg_ref, mq_ref, mk_ref, mv_ref, ga_ref, gb_ref, kmean_ref):
    hb = h_ref[...].astype(BF16)

    def proj(w_ref):
        return jnp.dot(hb, w_ref[...], preferred_element_type=F32)

    mcos, msa, msb = mcos_ref[...], msa_ref[...], msb_ref[...]

    def ret_rot(x, cos_ref, sin_ref):
        parts = []
        for hd in range(RET_HEADS):
            cols = slice(hd * RET_DK, (hd + 1) * RET_DK)
            xh = x[:, cols]
            parts.append(xh * cos_ref[:, cols] + pltpu.roll(xh, RET_DK // 2, 1) * sin_ref[:, cols])
        return jnp.concatenate(parts, axis=1)

    def moba_rot(x):
        parts = []
        half = ROPE_DIMS // 2
        for c in range(x.shape[1] // LANES):
            xc = x[:, c * LANES:(c + 1) * LANES]
            parts.append(xc * mcos + pltpu.roll(xc, LANES - half, 1) * msa + pltpu.roll(xc, half, 1) * msb)
        return jnp.concatenate(parts, axis=1)

    rq_ref[...] = ret_rot(proj(wrq_ref), qcos_ref, qsin_ref).astype(rq_ref.dtype)
    rk_ref[...] = ret_rot(proj(wrk_ref), kcos_ref, ksin_ref).astype(rk_ref.dtype)
    rv_ref[...] = proj(wrv_ref).astype(rv_ref.dtype)
    sg_ref[...] = _silu(proj(wrg_ref)).astype(sg_ref.dtype)
    mq_ref[...] = moba_rot(proj(wmq_ref))
    mk = moba_rot(proj(wmk_ref))
    mk_ref[...] = mk.astype(mk_ref.dtype)
    for r in range(kmean_ref.shape[0]):
        kmean_ref[r] = jnp.mean(mk[r * MOBA_BLOCK:(r + 1) * MOBA_BLOCK], axis=0, keepdims=True)
    mv_ref[...] = proj(wmv_ref).astype(mv_ref.dtype)
    ga_ref[...] = jax.nn.sigmoid(proj(wga_ref)).astype(ga_ref.dtype)
    gb_ref[...] = jax.nn.sigmoid(proj(wgb_ref)).astype(gb_ref.dtype)


def _rotary_tables(S):
    pos = jnp.arange(S).astype(F32)
    inv = 1.0 / (RET_ROT_BASE ** jnp.linspace(0.0, 1.0, RET_DK // 2, dtype=F32))
    ang = pos[:, None] * inv[None, :]
    cos, sin = jnp.cos(ang), jnp.sin(ang)
    rcos = jnp.concatenate([cos, cos], axis=1)
    rsin = jnp.concatenate([-sin, sin], axis=1)
    log_g = jnp.log(1.0 - 2.0 ** (-5.0 - jnp.arange(RET_HEADS, dtype=F32)))
    idx1 = (jnp.arange(S) % RET_CHUNK).astype(F32) + 1.0
    qdec = jnp.exp(log_g[None, :] * idx1[:, None])
    kdec = jnp.exp(-log_g[None, :] * idx1[:, None]) * (RET_DK ** -0.5)

    def per_head(tab, dec):
        return (tab[:, None, :] * dec[:, :, None]).reshape(S, RET_HEADS * RET_DK)

    qcos, qsin = per_head(rcos, qdec), per_head(rsin, qdec)
    kcos, ksin = per_head(rcos, kdec), per_head(rsin, kdec)
    inv = 1.0 / (ROPE_THETA ** (jnp.arange(0, ROPE_DIMS, 2, dtype=F32) / ROPE_DIMS))
    ang = pos[:, None] * inv[None, :]
    cos, sin = jnp.cos(ang), jnp.sin(ang)
    half = ROPE_DIMS // 2
    rest = MOBA_DH - ROPE_DIMS
    ones = jnp.ones((S, rest), F32)
    zeros = jnp.zeros((S, rest), F32)
    zh = jnp.zeros((S, half), F32)
    reps = LANES // MOBA_DH
    mcos = jnp.tile(jnp.concatenate([cos, cos, ones], axis=1), (1, reps))
    msa = jnp.tile(jnp.concatenate([-sin, zh, zeros], axis=1), (1, reps))
    msb = jnp.tile(jnp.concatenate([zh, sin, zeros], axis=1), (1, reps))
    return qcos, qsin, kcos, ksin, mcos, msa, msb


def _in_proj(h, w_in, S, *, tm):
    T, D = h.shape
    nqk = RET_HEADS * RET_DK
    nv = RET_HEADS * RET_DV
    nm = MOBA_HEADS * MOBA_DH
    assert tm % MOBA_BLOCK == 0 and S % tm == 0
    sizes = (nqk, nqk, nv, nv, nm, nm, nm, D, D)
    offs = [0]
    for s in sizes:
        offs.append(offs[-1] + s)
    assert offs[-1] == w_in.shape[1]
    wb = w_in.astype(BF16)
    ws = [wb[:, offs[i]:offs[i + 1]] for i in range(len(sizes))]
    tabs = _rotary_tables(S)
    spt = S // tm
    ret_tab = pl.BlockSpec((tm, nqk), lambda i: (i % spt, 0))
    moba_tab = pl.BlockSpec((tm, LANES), lambda i: (i % spt, 0))
    out_dtypes = (BF16, BF16, BF16, BF16, F32, BF16, BF16, BF16, BF16)
    bpt = tm // MOBA_BLOCK
    return pl.pallas_call(
        _in_proj_kernel,
        grid=(T // tm,),
        in_specs=[pl.BlockSpec((tm, D), lambda i: (i, 0))]
        + [_resident((D, s)) for s in sizes]
        + [ret_tab] * 4 + [moba_tab] * 3,
        out_specs=[pl.BlockSpec((tm, s), lambda i: (i, 0)) for s in sizes]
        + [pl.BlockSpec((bpt, 1, nm), lambda i: (i, 0, 0))],
        out_shape=[jax.ShapeDtypeStruct((T, s), dt) for s, dt in zip(sizes, out_dtypes)]
        + [jax.ShapeDtypeStruct((T // MOBA_BLOCK, 1, nm), F32)],
        compiler_params=_cparams("parallel"),
        name="in_proj",
    )(h, *ws, *tabs)


def _retention_kernel(q_ref, k_ref, v_ref, sg_ref, o_ref, state_ref, *, chunks_per_step):
    C = RET_CHUNK

    @pl.when(pl.program_id(1) == 0)
    def _():
        state_ref[...] = jnp.zeros_like(state_ref)

    row = lax.broadcasted_iota(jnp.int32, (C, C), 0)
    col = lax.broadcasted_iota(jnp.int32, (C, C), 1)
    tril = col <= row
    for c in range(chunks_per_step):
        rows = slice(c * C, (c + 1) * C)
        for h in range(RET_HEADS):
            kcols = slice(h * RET_DK, (h + 1) * RET_DK)
            vcols = slice(h * RET_DV, (h + 1) * RET_DV)
            q = q_ref[0, rows, kcols]
            k = k_ref[0, rows, kcols]
            v = v_ref[0, rows, vcols]
            state = state_ref[h]
            scores = lax.dot_general(q, k, NT_DIMS, preferred_element_type=F32)
            scores = jnp.where(tril, scores, 0.0).astype(BF16)
            lhs = jnp.concatenate([scores, q], axis=1)
            rhs = jnp.concatenate([v, state.astype(BF16)], axis=0)
            out = jnp.dot(lhs, rhs, preferred_element_type=F32)
            kv = lax.dot_general(k, v, TN_DIMS, preferred_element_type=F32)
            state_ref[h] = (_ret_decay(h) ** C) * (state + kv)
            mu = jnp.mean(out, axis=-1, keepdims=True)
            oc = out - mu
            var = jnp.mean(oc * oc, axis=-1, keepdims=True)
            y = oc * lax.rsqrt(var + GN_EPS) * sg_ref[0, rows, vcols].astype(F32)
            o_ref[0, rows, vcols] = y.astype(o_ref.dtype)


def _retention(rq, rk, rv, sg, *, ts):
    B, S, _ = rq.shape
    nqk = RET_HEADS * RET_DK
    nv = RET_HEADS * RET_DV
    qk_spec = pl.BlockSpec((1, ts, nqk), lambda b, s: (b, s, 0))
    v_spec = pl.BlockSpec((1, ts, nv), lambda b, s: (b, s, 0))
    return pl.pallas_call(
        functools.partial(_retention_kernel, chunks_per_step=ts // RET_CHUNK),
        grid=(B, S // ts),
        in_specs=[qk_spec, qk_spec, v_spec, v_spec],
        out_specs=v_spec,
        out_shape=jax.ShapeDtypeStruct((B, S, nv), BF16),
        scratch_shapes=[pltpu.VMEM((RET_HEADS, RET_DK, RET_DV), F32)],
        compiler_params=_cparams("parallel", "arbitrary"),
        name="retention",
    )(rq, rk, rv, sg)


def _split_dot(a, b):
    a_hi = a.astype(BF16)
    a_lo = (a - a_hi.astype(F32)).astype(BF16)
    b_hi = b.astype(BF16)
    b_lo = (b - b_hi.astype(F32)).astype(BF16)
    dot = functools.partial(jnp.dot, preferred_element_type=F32)
    return dot(a_hi, b_hi) + (dot(a_hi, b_lo) + dot(a_lo, b_hi))


def _pad_rows(x, before, total):
    parts = []
    if before:
        parts.append(jnp.zeros((before, x.shape[1]), x.dtype))
    parts.append(x)
    after = total - before - x.shape[0]
    if after:
        parts.append(jnp.zeros((after, x.shape[1]), x.dtype))
    return jnp.concatenate(parts, axis=0)


def _moba_kernel(q_ref, k_ref, v_ref, kmean_ref, o_ref, *, n_blocks):
    BLK = MOBA_BLOCK
    D = MOBA_DH
    S = n_blocks * BLK
    scale = (D ** -0.5) * math.log2(math.e)
    q2 = q_ref[0]
    k2 = k_ref[0].astype(F32)
    v2 = v_ref[0]
    kmean2 = kmean_ref[0]
    lane = lax.broadcasted_iota(jnp.int32, (S, LANES), 1)
    key_blk = lax.broadcasted_iota(jnp.int32, (S, LANES), 0) // BLK
    row = lax.broadcasted_iota(jnp.int32, (BLK, BLK), 0)
    col = lax.broadcasted_iota(jnp.int32, (BLK, BLK), 1)
    causal = col <= row
    blk_t = lax.broadcasted_iota(jnp.int32, (n_blocks, S), 0)
    qblk_t = lax.broadcasted_iota(jnp.int32, (n_blocks, S), 1) // BLK
    out = None
    for hh in range(LANES // D):
        own = (lane >= hh * D) & (lane < (hh + 1) * D)
        aux0 = D - hh * D
        qm = jnp.where(own, q2, 0.0)
        km_rows = _pad_rows(jnp.where(own[:n_blocks], kmean2, 0.0), aux0, LANES)
        gate = _split_dot(qm, km_rows.T)
        gate_t = gate.T[aux0:aux0 + n_blocks]
        rank = jnp.zeros((n_blocks, S), jnp.int32)
        for jp in range(n_blocks - 1):
            gj = gate_t[jp:jp + 1]
            beats = (gj > gate_t) | ((gj == gate_t) & (jp < blk_t))
            rank = rank + jnp.where(beats & (jp < qblk_t), 1, 0)
        keep = ((rank < MOBA_TOPK) & (blk_t < qblk_t)) | (blk_t == qblk_t)
        sel_t = jnp.where(keep, 0.0, MASKED).astype(F32)
        q_aug = (qm * scale + _pad_rows(sel_t, aux0, LANES).T).astype(BF16)
        onehot = jnp.where(lane - aux0 == key_blk, 1.0, 0.0)
        k_aug = jnp.where(own, k2, onehot).astype(BF16)
        tiles = []
        for i in range(n_blocks):
            qi = q_aug[i * BLK:(i + 1) * BLK]
            own_rows = slice(i * BLK, (i + 1) * BLK)
            s_own = lax.dot_general(qi, k_aug[own_rows], NT_DIMS, preferred_element_type=F32)
            s_own = jnp.where(causal, s_own, MASKED)
            m = jnp.max(s_own, axis=-1, keepdims=True)
            if i > 0:
                s_past = lax.dot_general(qi, k_aug[:i * BLK], NT_DIMS, preferred_element_type=F32)
                m = jnp.maximum(m, jnp.max(s_past, axis=-1, keepdims=True))
            p_own = jnp.exp2(s_own - m)
            l = jnp.sum(p_own, axis=-1, keepdims=True)
            acc = jnp.dot(p_own.astype(BF16), v2[own_rows], preferred_element_type=F32)
            if i > 0:
                p_past = jnp.exp2(s_past - m)
                l = l + jnp.sum(p_past, axis=-1, keepdims=True)
                acc = acc + jnp.dot(p_past.astype(BF16), v2[:i * BLK], preferred_element_type=F32)
            tiles.append(acc / l)
        head_out = jnp.concatenate(tiles, axis=0)
        out = head_out if out is None else jnp.where(own, head_out, out)
    o_ref[0] = out.astype(o_ref.dtype)


def _moba(mq, mk, mv, kmean):
    B, S, W = mq.shape
    n_blocks = S // MOBA_BLOCK
    spec = pl.BlockSpec((1, S, LANES), lambda b, c: (b, 0, c))
    return pl.pallas_call(
        functools.partial(_moba_kernel, n_blocks=n_blocks),
        grid=(B, W // LANES),
        in_specs=[spec, spec, spec, pl.BlockSpec((1, n_blocks, LANES), lambda b, c: (b, 0, c))],
        out_specs=spec,
        out_shape=jax.ShapeDtypeStruct((B, S, W), BF16),
        compiler_params=_cparams("parallel", "parallel"),
        name="moba",
    )(mq, mk, mv, kmean)


def _mix_ln_kernel(h_ref, yr_ref, ym_ref, ga_ref, gb_ref, pa_ref, pb_ref, wo_ref, g_ref, b_ref, o_ref, *, alpha):
    ya = jnp.dot(yr_ref[...], pa_ref[...], preferred_element_type=F32)
    yb = jnp.dot(ym_ref[...], pb_ref[...], preferred_element_type=F32)
    mix = (ga_ref[...].astype(F32) * ya + gb_ref[...].astype(F32) * yb).astype(BF16)
    out = jnp.dot(mix, wo_ref[...], preferred_element_type=F32)
    o_ref[...] = _layer_norm(alpha * h_ref[...] + out, g_ref[...], b_ref[...])


def _mix_ln(h, y_ret, y_moba, ga, gb, ret_proj, moba_proj, w_out, g, b, *, alpha, tm):
    T, D = h.shape
    NR = y_ret.shape[1]
    NM = y_moba.shape[1]

    def tile(n):
        return pl.BlockSpec((tm, n), lambda i: (i, 0))

    return pl.pallas_call(
        functools.partial(_mix_ln_kernel, alpha=alpha),
        grid=(T // tm,),
        in_specs=[tile(D), tile(NR), tile(NM), tile(D), tile(D),
                  _resident((NR, D)), _resident((NM, D)), _resident((D, D)), _resident((1, D)), _resident((1, D))],
        out_specs=tile(D),
        out_shape=jax.ShapeDtypeStruct((T, D), F32),
        compiler_params=_cparams("parallel"),
        name="mix_ln",
    )(h, y_ret, y_moba, ga, gb, ret_proj.astype(BF16), moba_proj.astype(BF16), w_out.astype(BF16),
      g.reshape(1, D), b.reshape(1, D))


def kernel(x, ln1_g, ln1_b, ffn1_w_gu, ffn1_w_down, w_in, ret_proj, moba_proj, w_out, lnm_g, lnm_b,
           ffn2_w_gu, ffn2_w_down, ln2_g, ln2_b):
    B, S, D = x.shape
    depth = ffn1_w_gu.shape[0]
    alpha = (2.0 * depth) ** 0.25
    T = B * S
    h = x.reshape(T, D)
    for l in range(depth):
        h = _ffn_ln(h, ffn1_w_gu[l], ffn1_w_down[l], ln1_g[l], ln1_b[l], alpha=alpha, tm=512)
        rq, rk, rv, sg, mq, mk, mv, ga, gb, kmean = _in_proj(h, w_in[l], S, tm=512)
        y_ret = _retention(rq.reshape(B, S, -1), rk.reshape(B, S, -1), rv.reshape(B, S, -1), sg.reshape(B, S, -1),
                           ts=512)
        y_moba = _moba(mq.reshape(B, S, -1), mk.reshape(B, S, -1), mv.reshape(B, S, -1),
                       kmean.reshape(B, S // MOBA_BLOCK, -1))
        h = _mix_ln(h, y_ret.reshape(T, -1), y_moba.reshape(T, -1), ga, gb, ret_proj[l], moba_proj[l], w_out[l],
                    lnm_g[l], lnm_b[l], alpha=alpha, tm=512)
        h = _ffn_ln(h, ffn2_w_gu[l], ffn2_w_down[l], ln2_g[l], ln2_b[l], alpha=alpha, tm=512)
    return h.reshape(B, S, D)
```

```python
import functools
import math

import jax
import jax.numpy as jnp
from jax import lax
from jax.experimental import pallas as pl
from jax.experimental.pallas import tpu as pltpu

F32 = jnp.float32
BF16 = jnp.bfloat16

RET_HEADS = 4
RET_DK = 128
RET_DV = 256
RET_CHUNK = 128
RET_ROT_BASE = 10000.0
MOBA_HEADS = 8
MOBA_DH = 64
MOBA_BLOCK = 256
MOBA_TOPK = 3
ROPE_THETA = 500000.0
ROPE_DIMS = MOBA_DH // 4
LN_EPS = 1e-5
GN_EPS = 1e-5

V7X_VMEM_LIMIT_BYTES = 56 * 1024 * 1024
LANES = 128
SUBLANES = 8
ONES_ROWS = 16
MASKED = -1e30

NT_DIMS = (((1,), (1,)), ((), ()))
TN_DIMS = (((0,), (0,)), ((), ()))


def _cparams(*sem):
    return pltpu.CompilerParams(dimension_semantics=sem, vmem_limit_bytes=V7X_VMEM_LIMIT_BYTES)


def _resident(shape):
    nd = len(shape)
    return pl.BlockSpec(shape, lambda *_: (0,) * nd, pipeline_mode=pl.Buffered(1))


def _layer_norm(y, g, b):
    mu = jnp.mean(y, axis=-1, keepdims=True)
    yc = y - mu
    var = jnp.mean(yc * yc, axis=-1, keepdims=True)
    return yc * lax.rsqrt(var + LN_EPS) * g + b


def _silu(x):
    return x * jax.nn.sigmoid(x)


def _ret_decay(h):
    return 1.0 - 2.0 ** (-5.0 - h)


def _ffn_ln_kernel(x_ref, wg_ref, wu_ref, wd_ref, g_ref, b_ref, o_ref, *, alpha):
    x = x_ref[...]
    xb = x.astype(BF16)
    gate = jnp.dot(xb, wg_ref[...], preferred_element_type=F32)
    up = jnp.dot(xb, wu_ref[...], preferred_element_type=F32)
    mid = (_silu(gate) * up).astype(BF16)
    ffn = jnp.dot(mid, wd_ref[...], preferred_element_type=F32)
    o_ref[...] = _layer_norm(alpha * x + 0.5 * ffn, g_ref[...], b_ref[...])


def _ffn_ln(x, w_gu, w_down, g, b, *, alpha, tm):
    T, D = x.shape
    FF = w_down.shape[0]
    wg = w_gu[:, :FF].astype(BF16)
    wu = w_gu[:, FF:].astype(BF16)
    wd = w_down.astype(BF16)
    return pl.pallas_call(
        functools.partial(_ffn_ln_kernel, alpha=alpha),
        grid=(T // tm,),
        in_specs=[
            pl.BlockSpec((tm, D), lambda i: (i, 0)),
            _resident((D, FF)),
            _resident((D, FF)),
            _resident((FF, D)),
            _resident((1, D)),
            _resident((1, D)),
        ],
        out_specs=pl.BlockSpec((tm, D), lambda i: (i, 0)),
        out_shape=jax.ShapeDtypeStruct((T, D), F32),
        compiler_params=_cparams("parallel"),
        name="ffn_ln",
    )(x, wg, wu, wd, g.reshape(1, D), b.reshape(1, D))


def _in_proj_kernel(h_ref, wrq_ref, wrk_ref, wrv_ref, wrg_ref, wmq_ref, wmk_ref, wmv_ref, wga_ref, wgb_ref,
                    qcos_ref, qsin_ref, kcos_ref, ksin_ref, mcos_ref, msa_ref, msb_ref,
                    rq_ref, rk_ref, rv_ref, sg_ref, mq_ref, mk_ref, mv_ref, ga_ref, gb_ref, kmean_ref):
    hb = h_ref[...].astype(BF16)

    def proj(w_ref):
        return jnp.dot(hb, w_ref[...], preferred_element_type=F32)

    mcos, msa, msb = mcos_ref[...], msa_ref[...], msb_ref[...]

    def ret_rot(x, cos_ref, sin_ref):
        parts = []
        for hd in range(RET_HEADS):
            cols = slice(hd * RET_DK, (hd + 1) * RET_DK)
            xh = x[:, cols]
            parts.append(xh * cos_ref[:, cols] + pltpu.roll(xh, RET_DK // 2, 1) * sin_ref[:, cols])
        return jnp.concatenate(parts, axis=1)

    def moba_rot(x):
        parts = []
        half = ROPE_DIMS // 2
        for c in range(x.shape[1] // LANES):
            xc = x[:, c * LANES:(c + 1) * LANES]
            parts.append(xc * mcos + pltpu.roll(xc, LANES - half, 1) * msa + pltpu.roll(xc, half, 1) * msb)
        return jnp.concatenate(parts, axis=1)

    rq_ref[...] = ret_rot(proj(wrq_ref), qcos_ref, qsin_ref).astype(rq_ref.dtype)
    rk_ref[...] = ret_rot(proj(wrk_ref), kcos_ref, ksin_ref).astype(rk_ref.dtype)
    rv_ref[...] = proj(wrv_ref).astype(rv_ref.dtype)
    sg_ref[...] = _silu(proj(wrg_ref)).astype(sg_ref.dtype)
    mq_ref[...] = moba_rot(proj(wmq_ref))
    mk = moba_rot(proj(wmk_ref))
    mk_ref[...] = mk.astype(mk_ref.dtype)
    for r in range(kmean_ref.shape[0]):
        kmean_ref[r] = jnp.mean(mk[r * MOBA_BLOCK:(r + 1) * MOBA_BLOCK], axis=0, keepdims=True)
    mv_ref[...] = proj(wmv_ref).astype(mv_ref.dtype)
    ga_ref[...] = jax.nn.sigmoid(proj(wga_ref)).astype(ga_ref.dtype)
    gb_ref[...] = jax.nn.sigmoid(proj(wgb_ref)).astype(gb_ref.dtype)


def _rotary_tables(S):
    pos = jnp.arange(S).astype(F32)
    inv = 1.0 / (RET_ROT_BASE ** jnp.linspace(0.0, 1.0, RET_DK // 2, dtype=F32))
    ang = pos[:, None] * inv[None, :]
    cos, sin = jnp.cos(ang), jnp.sin(ang)
    rcos = jnp.concatenate([cos, cos], axis=1)
    rsin = jnp.concatenate([-sin, sin], axis=1)
    log_g = jnp.log(1.0 - 2.0 ** (-5.0 - jnp.arange(RET_HEADS, dtype=F32)))
    idx1 = (jnp.arange(S) % RET_CHUNK).astype(F32) + 1.0
    qdec = jnp.exp(log_g[None, :] * idx1[:, None])
    kdec = jnp.exp(-log_g[None, :] * idx1[:, None]) * (RET_DK ** -0.5)

    def per_head(tab, dec):
        return (tab[:, None, :] * dec[:, :, None]).reshape(S, RET_HEADS * RET_DK)

    qcos, qsin = per_head(rcos, qdec), per_head(rsin, qdec)
    kcos, ksin = per_head(rcos, kdec), per_head(rsin, kdec)
    inv = 1.0 / (ROPE_THETA ** (jnp.arange(0, ROPE_DIMS, 2, dtype=F32) / ROPE_DIMS))
    ang = pos[:, None] * inv[None, :]
    cos, sin = jnp.cos(ang), jnp.sin(ang)
    half = ROPE_DIMS // 2
    rest = MOBA_DH - ROPE_DIMS
    ones = jnp.ones((S, rest), F32)
    zeros = jnp.zeros((S, rest), F32)
    zh = jnp.zeros((S, half), F32)
    reps = LANES // MOBA_DH
    mcos = jnp.tile(jnp.concatenate([cos, cos, ones], axis=1), (1, reps))
    msa = jnp.tile(jnp.concatenate([-sin, zh, zeros], axis=1), (1, reps))
    msb = jnp.tile(jnp.concatenate([zh, sin, zeros], axis=1), (1, reps))
    return qcos, qsin, kcos, ksin, mcos, msa, msb


def _in_proj(h, w_in, S, *, tm):
    T, D = h.shape
    nqk = RET_HEADS * RET_DK
    nv = RET_HEADS * RET_DV
    nm = MOBA_HEADS * MOBA_DH
    assert tm % MOBA_BLOCK == 0 and S % tm == 0
    sizes = (nqk, nqk, nv, nv, nm, nm, nm, D, D)
    offs = [0]
    for s in sizes:
        offs.append(offs[-1] + s)
    assert offs[-1] == w_in.shape[1]
    wb = w_in.astype(BF16)
    ws = [wb[:, offs[i]:offs[i + 1]] for i in range(len(sizes))]
    tabs = _rotary_tables(S)
    spt = S // tm
    ret_tab = pl.BlockSpec((tm, nqk), lambda i: (i % spt, 0))
    moba_tab = pl.BlockSpec((tm, LANES), lambda i: (i % spt, 0))
    out_dtypes = (BF16, BF16, BF16, BF16, F32, BF16, BF16, BF16, BF16)
    bpt = tm // MOBA_BLOCK
    return pl.pallas_call(
        _in_proj_kernel,
        grid=(T // tm,),
        in_specs=[pl.BlockSpec((tm, D), lambda i: (i, 0))]
        + [_resident((D, s)) for s in sizes]
        + [ret_tab] * 4 + [moba_tab] * 3,
        out_specs=[pl.BlockSpec((tm, s), lambda i: (i, 0)) for s in sizes]
        + [pl.BlockSpec((bpt, 1, nm), lambda i: (i, 0, 0))],
        out_shape=[jax.ShapeDtypeStruct((T, s), dt) for s, dt in zip(sizes, out_dtypes)]
        + [jax.ShapeDtypeStruct((T // MOBA_BLOCK, 1, nm), F32)],
        compiler_params=_cparams("parallel"),
        name="in_proj",
    )(h, *ws, *tabs)


def _retention_kernel(q_ref, k_ref, v_ref, sg_ref, o_ref, state_ref, *, chunks_per_step):
    C = RET_CHUNK

    @pl.when(pl.program_id(1) == 0)
    def _():
        state_ref[...] = jnp.zeros_like(state_ref)

    row = lax.broadcasted_iota(jnp.int32, (C, C), 0)
    col = lax.broadcasted_iota(jnp.int32, (C, C), 1)
    tril = col <= row
    for c in range(chunks_per_step):
        rows = slice(c * C, (c + 1) * C)
        for h in range(RET_HEADS):
            kcols = slice(h * RET_DK, (h + 1) * RET_DK)
            vcols = slice(h * RET_DV, (h + 1) * RET_DV)
            q = q_ref[0, rows, kcols]
            k = k_ref[0, rows, kcols]
            v = v_ref[0, rows, vcols]
            state = state_ref[h]
            scores = lax.dot_general(q, k, NT_DIMS, preferred_element_type=F32)
            scores = jnp.where(tril, scores, 0.0).astype(BF16)
            lhs = jnp.concatenate([scores, q], axis=1)
            rhs = jnp.concatenate([v, state.astype(BF16)], axis=0)
            out = jnp.dot(lhs, rhs, preferred_element_type=F32)
            kv = lax.dot_general(k, v, TN_DIMS, preferred_element_type=F32)
            state_ref[h] = (_ret_decay(h) ** C) * (state + kv)
            mu = jnp.mean(out, axis=-1, keepdims=True)
            oc = out - mu
            var = jnp.mean(oc * oc, axis=-1, keepdims=True)
            y = oc * lax.rsqrt(var + GN_EPS) * sg_ref[0, rows, vcols].astype(F32)
            o_ref[0, rows, vcols] = y.astype(o_ref.dtype)


def _retention(rq, rk, rv, sg, *, ts):
    B, S, _ = rq.shape
    nqk = RET_HEADS * RET_DK
    nv = RET_HEADS * RET_DV
    qk_spec = pl.BlockSpec((1, ts, nqk), lambda b, s: (b, s, 0))
    v_spec = pl.BlockSpec((1, ts, nv), lambda b, s: (b, s, 0))
    return pl.pallas_call(
        functools.partial(_retention_kernel, chunks_per_step=ts // RET_CHUNK),
        grid=(B, S // ts),
        in_specs=[qk_spec, qk_spec, v_spec, v_spec],
        out_specs=v_spec,
        out_shape=jax.ShapeDtypeStruct((B, S, nv), BF16),
        scratch_shapes=[pltpu.VMEM((RET_HEADS, RET_DK, RET_DV), F32)],
        compiler_params=_cparams("parallel", "arbitrary"),
        name="retention",
    )(rq, rk, rv, sg)


def _split_dot(a, b):
    a_hi = a.astype(BF16)
    a_lo = (a - a_hi.astype(F32)).astype(BF16)
    b_hi = b.astype(BF16)
    b_lo = (b - b_hi.astype(F32)).astype(BF16)
    dot = functools.partial(jnp.dot, preferred_element_type=F32)
    return dot(a_hi, b_hi) + (dot(a_hi, b_lo) + dot(a_lo, b_hi))


def _moba_kernel(q_ref, k_ref, v_ref, kmean_ref, o_ref, *, n_blocks):
    BLK = MOBA_BLOCK
    D = MOBA_DH
    S = n_blocks * BLK
    scale = (D ** -0.5) * math.log2(math.e)
    n_heads = LANES // D
    q_t = q_ref[0].T
    k2 = k_ref[0].astype(F32)
    v_t = v_ref[0].astype(F32).T.astype(BF16)
    kmean2 = kmean_ref[0]
    lane = lax.broadcasted_iota(jnp.int32, (S, LANES), 1)
    key_blk = lax.broadcasted_iota(jnp.int32, (S, LANES), 0) // BLK
    key_row = lax.broadcasted_iota(jnp.int32, (BLK, BLK), 0)
    query_col = lax.broadcasted_iota(jnp.int32, (BLK, BLK), 1)
    causal = key_row <= query_col
    blk_t = lax.broadcasted_iota(jnp.int32, (n_blocks, S), 0)
    qblk_t = lax.broadcasted_iota(jnp.int32, (n_blocks, S), 1) // BLK
    own_lanes = [(lane[:n_blocks] >= hh * D) & (lane[:n_blocks] < (hh + 1) * D) for hh in range(n_heads)]
    km_all = jnp.concatenate([jnp.where(own_lanes[hh], kmean2, 0.0) for hh in range(n_heads)], axis=0)
    gate_all = _split_dot(km_all, q_t)
    heads = []
    for hh in range(n_heads):
        gate_t = gate_all[hh * n_blocks:(hh + 1) * n_blocks]
        rank = jnp.zeros((n_blocks, S), jnp.int32)
        for jp in range(n_blocks - 1):
            gj = gate_t[jp:jp + 1]
            beats = (gj > gate_t) | ((gj == gate_t) & (jp < blk_t))
            rank = rank + jnp.where(beats & (jp < qblk_t), 1, 0)
        keep = ((rank < MOBA_TOPK) & (blk_t < qblk_t)) | (blk_t == qblk_t)
        sel_t = jnp.where(keep, 0.0, MASKED).astype(F32)
        own = (lane >= hh * D) & (lane < (hh + 1) * D)
        aux0 = D - hh * D
        qd = q_t[hh * D:(hh + 1) * D] * scale
        pad = jnp.zeros((LANES - D - n_blocks, S), F32)
        parts = [qd, sel_t, pad] if hh == 0 else [sel_t, pad, qd]
        q_aug_t = jnp.concatenate(parts, axis=0).astype(BF16)
        onehot = jnp.where(lane - aux0 == key_blk, 1.0, 0.0)
        k_aug = jnp.where(own, k2, onehot).astype(BF16)
        v_aug_t = jnp.concatenate([v_t[hh * D:(hh + 1) * D], jnp.ones((ONES_ROWS, S), BF16)], axis=0)
        heads.append((q_aug_t, k_aug, v_aug_t))

    def fold8(x, op):
        return op(x.reshape(BLK // SUBLANES, SUBLANES, BLK), axis=0)

    def logits_steps(hh, i, st):
        q_aug_t, k_aug, _ = heads[hh]
        q_i = q_aug_t[:, i * BLK:(i + 1) * BLK]
        st["s"] = []

        def step(j):
            s = jnp.dot(k_aug[j * BLK:(j + 1) * BLK], q_i, preferred_element_type=F32)
            if j == i:
                s = jnp.where(causal, s, MASKED)
            st["s"].append(s)
            m8 = fold8(s, jnp.max)
            st["m8"] = m8 if "m8" not in st else jnp.maximum(st["m8"], m8)

        return [functools.partial(step, j) for j in range(i, -1, -1)]

    def weight_steps(hh, i, st):
        v_h = heads[hh][2]

        def step(n):
            j = i - n
            if n == 0:
                st["m"] = jnp.max(st["m8"], axis=0, keepdims=True)
            p = jnp.exp2(st["s"][n] - st["m"])
            pv = jnp.dot(v_h[:, j * BLK:(j + 1) * BLK], p.astype(BF16), preferred_element_type=F32)
            st["acc"] = pv if n == 0 else st["acc"] + pv
            if n == i:
                st["out"] = st["acc"][:D] / st["acc"][D:D + 1]

        return [functools.partial(step, n) for n in range(i + 1)]

    def zipped(lists):
        return [step for group in zip(*lists) for step in group]

    states = [dict() for _ in range(n_blocks * n_heads)]
    pending = []
    for i in range(n_blocks):
        group = [states[i * n_heads + hh] for hh in range(n_heads)]
        fresh = zipped([logits_steps(hh, i, group[hh]) for hh in range(n_heads)])
        while fresh or pending:
            if fresh:
                fresh.pop(0)()
            if pending:
                pending.pop(0)()
        pending = zipped([weight_steps(hh, i, group[hh]) for hh in range(n_heads)])
    for step in pending:
        step()
    out_t = jnp.concatenate(
        [jnp.concatenate([states[i * n_heads + hh]["out"] for i in range(n_blocks)], axis=1) for hh in range(n_heads)],
        axis=0)
    o_ref[0] = out_t.T.astype(o_ref.dtype)


def _moba(mq, mk, mv, kmean):
    B, S, W = mq.shape
    n_blocks = S // MOBA_BLOCK
    spec = pl.BlockSpec((1, S, LANES), lambda b, c: (b, 0, c))
    return pl.pallas_call(
        functools.partial(_moba_kernel, n_blocks=n_blocks),
        grid=(B, W // LANES),
        in_specs=[spec, spec, spec, pl.BlockSpec((1, n_blocks, LANES), lambda b, c: (b, 0, c))],
        out_specs=spec,
        out_shape=jax.ShapeDtypeStruct((B, S, W), BF16),
        compiler_params=_cparams("parallel", "parallel"),
        name="moba",
    )(mq, mk, mv, kmean)


def _mix_ln_kernel(h_ref, yr_ref, ym_ref, ga_ref, gb_ref, pa_ref, pb_ref, wo_ref, g_ref, b_ref, o_ref, *, alpha):
    ya = jnp.dot(yr_ref[...], pa_ref[...], preferred_element_type=F32)
    yb = jnp.dot(ym_ref[...], pb_ref[...], preferred_element_type=F32)
    mix = (ga_ref[...].astype(F32) * ya + gb_ref[...].astype(F32) * yb).astype(BF16)
    out = jnp.dot(mix, wo_ref[...], preferred_element_type=F32)
    o_ref[...] = _layer_norm(alpha * h_ref[...] + out, g_ref[...], b_ref[...])


def _mix_ln(h, y_ret, y_moba, ga, gb, ret_proj, moba_proj, w_out, g, b, *, alpha, tm):
    T, D = h.shape
    NR = y_ret.shape[1]
    NM = y_moba.shape[1]

    def tile(n):
        return pl.BlockSpec((tm, n), lambda i: (i, 0))

    return pl.pallas_call(
        functools.partial(_mix_ln_kernel, alpha=alpha),
        grid=(T // tm,),
        in_specs=[tile(D), tile(NR), tile(NM), tile(D), tile(D),
                  _resident((NR, D)), _resident((NM, D)), _resident((D, D)), _resident((1, D)), _resident((1, D))],
        out_specs=tile(D),
        out_shape=jax.ShapeDtypeStruct((T, D), F32),
        compiler_params=_cparams("parallel"),
        name="mix_ln",
    )(h, y_ret, y_moba, ga, gb, ret_proj.astype(BF16), moba_proj.astype(BF16), w_out.astype(BF16),
      g.reshape(1, D), b.reshape(1, D))


def kernel(x, ln1_g, ln1_b, ffn1_w_gu, ffn1_w_down, w_in, ret_proj, moba_proj, w_out, lnm_g, lnm_b,
           ffn2_w_gu, ffn2_w_down, ln2_g, ln2_b):
    B, S, D = x.shape
    depth = ffn1_w_gu.shape[0]
    alpha = (2.0 * depth) ** 0.25
    T = B * S
    h = x.reshape(T, D)
    for l in range(depth):
        h = _ffn_ln(h, ffn1_w_gu[l], ffn1_w_down[l], ln1_g[l], ln1_b[l], alpha=alpha, tm=512)
        rq, rk, rv, sg, mq, mk, mv, ga, gb, kmean = _in_proj(h, w_in[l], S, tm=512)
        y_ret = _retention(rq.reshape(B, S, -1), rk.reshape(B, S, -1), rv.reshape(B, S, -1), sg.reshape(B, S, -1),
                           ts=512)
        y_moba = _moba(mq.reshape(B, S, -1), mk.reshape(B, S, -1), mv.reshape(B, S, -1),
                       kmean.reshape(B, S // MOBA_BLOCK, -1))
        h = _mix_ln(h, y_ret.reshape(T, -1), y_moba.reshape(T, -1), ga, gb, ret_proj[l], moba_proj[l], w_out[l],
                    lnm_g[l], lnm_b[l], alpha=alpha, tm=512)
        h = _ffn_ln(h, ffn2_w_gu[l], ffn2_w_down[l], ln2_g[l], ln2_b[l], alpha=alpha, tm=512)
    return h.reshape(B, S, D)
```

```python
import functools
import math

import jax
import jax.numpy as jnp
from jax import lax
from jax.experimental import pallas as pl
from jax.experimental.pallas import tpu as pltpu

F32 = jnp.float32
BF16 = jnp.bfloat16

RET_HEADS = 4
RET_DK = 128
RET_DV = 256
RET_CHUNK = 128
RET_ROT_BASE = 10000.0
MOBA_HEADS = 8
MOBA_DH = 64
MOBA_BLOCK = 256
MOBA_TOPK = 3
ROPE_THETA = 500000.0
ROPE_DIMS = MOBA_DH // 4
LN_EPS = 1e-5
GN_EPS = 1e-5

V7X_VMEM_LIMIT_BYTES = 56 * 1024 * 1024
LANES = 128
SUBLANES = 8
ONES_ROWS = 16
FFN_TILE_ROWS = 1024
FFN_SUB_ROWS = 256
FFN_CHUNK = 512
IN_PROJ_TILE_ROWS = 512
RET_TILE_ROWS = 1024
MIX_TILE_ROWS = 1024
MIX_SUB_ROWS = 256
MOBA_KEY_STEP = 256
MASKED = -1e30

NT_DIMS = (((1,), (1,)), ((), ()))
TN_DIMS = (((0,), (0,)), ((), ()))


def _cparams(*sem):
    return pltpu.CompilerParams(dimension_semantics=sem, vmem_limit_bytes=V7X_VMEM_LIMIT_BYTES)


def _resident(shape):
    nd = len(shape)
    return pl.BlockSpec(shape, lambda *_: (0,) * nd, pipeline_mode=pl.Buffered(1))


def _layer_norm(y, g, b):
    mu = jnp.mean(y, axis=-1, keepdims=True)
    yc = y - mu
    var = jnp.mean(yc * yc, axis=-1, keepdims=True)
    return yc * lax.rsqrt(var + LN_EPS) * g + b


def _silu(x):
    return x * jax.nn.sigmoid(x)


def _ret_decay(h):
    return 1.0 - 2.0 ** (-5.0 - h)


def _ffn_ln_kernel(x_ref, wg_ref, wu_ref, wd_ref, g_ref, b_ref, o_ref, *, alpha):
    FF = wd_ref.shape[0]
    chunks = [slice(c, min(c + FFN_CHUNK, FF)) for c in range(0, FF, FFN_CHUNK)]
    subs = [slice(r, r + FFN_SUB_ROWS) for r in range(0, x_ref.shape[0], FFN_SUB_ROWS)]

    def gate_up(xb, cols):
        return (jnp.dot(xb, wg_ref[:, cols], preferred_element_type=F32),
                jnp.dot(xb, wu_ref[:, cols], preferred_element_type=F32))

    def finish(rows, x, ffn):
        o_ref[rows, :] = _layer_norm(alpha * x + 0.5 * ffn, g_ref[...], b_ref[...])

    unfinished = None
    for rows in subs:
        x = x_ref[rows, :]
        xb = x.astype(BF16)
        ffn = None
        ahead = gate_up(xb, chunks[0])
        if unfinished is not None:
            finish(*unfinished)
        for n, cols in enumerate(chunks):
            gate, up = ahead
            if n + 1 < len(chunks):
                ahead = gate_up(xb, chunks[n + 1])
            mid = (_silu(gate) * up).astype(BF16)
            part = jnp.dot(mid, wd_ref[cols, :], preferred_element_type=F32)
            ffn = part if ffn is None else ffn + part
        unfinished = (rows, x, ffn)
    finish(*unfinished)


def _ffn_ln(x, w_gu, w_down, g, b, *, alpha, tm):
    T, D = x.shape
    FF = w_down.shape[0]
    wg = w_gu[:, :FF].astype(BF16)
    wu = w_gu[:, FF:].astype(BF16)
    wd = w_down.astype(BF16)
    return pl.pallas_call(
        functools.partial(_ffn_ln_kernel, alpha=alpha),
        grid=(T // tm,),
        in_specs=[
            pl.BlockSpec((tm, D), lambda i: (i, 0)),
            _resident((D, FF)),
            _resident((D, FF)),
            _resident((FF, D)),
            _resident((1, D)),
            _resident((1, D)),
        ],
        out_specs=pl.BlockSpec((tm, D), lambda i: (i, 0)),
        out_shape=jax.ShapeDtypeStruct((T, D), F32),
        compiler_params=_cparams("parallel"),
        name="ffn_ln",
    )(x, wg, wu, wd, g.reshape(1, D), b.reshape(1, D))


def _in_proj_kernel(h_ref, wrq_ref, wrk_ref, wrv_ref, wrg_ref, wmq_ref, wmk_ref, wmv_ref, wga_ref, wgb_ref,
                    qcos_ref, qsin_ref, kcos_ref, ksin_ref, mcos_ref, msa_ref, msb_ref,
                    rq_ref, rk_ref, rv_ref, sg_ref, mq_ref, mk_ref, mv_ref, ga_ref, gb_ref, kmean_ref):
    half = ROPE_DIMS // 2

    def ret_rot(x, rows, cos_ref, sin_ref):
        parts = []
        for hd in range(RET_HEADS):
            cols = slice(hd * RET_DK, (hd + 1) * RET_DK)
            xh = x[:, cols]
            parts.append(xh * cos_ref[rows, cols] + pltpu.roll(xh, RET_DK // 2, 1) * sin_ref[rows, cols])
        return jnp.concatenate(parts, axis=1)

    def moba_rot(x, rows):
        parts = []
        for c in range(x.shape[1] // LANES):
            xc = x[:, c * LANES:(c + 1) * LANES]
            parts.append(xc * mcos_ref[rows, :] + pltpu.roll(xc, LANES - half, 1) * msa_ref[rows, :]
                         + pltpu.roll(xc, half, 1) * msb_ref[rows, :])
        return jnp.concatenate(parts, axis=1)

    def put(o_ref):
        def post(y, rows, r):
            o_ref[rows, :] = y.astype(o_ref.dtype)
        return post

    def put_moba_k(y, rows, r):
        mk = moba_rot(y, rows)
        mk_ref[rows, :] = mk.astype(mk_ref.dtype)
        kmean_ref[r] = jnp.mean(mk, axis=0, keepdims=True)

    segments = [
        (wrq_ref, lambda y, rows, r: put(rq_ref)(ret_rot(y, rows, qcos_ref, qsin_ref), rows, r)),
        (wrk_ref, lambda y, rows, r: put(rk_ref)(ret_rot(y, rows, kcos_ref, ksin_ref), rows, r)),
        (wrv_ref, put(rv_ref)),
        (wrg_ref, lambda y, rows, r: put(sg_ref)(_silu(y), rows, r)),
        (wmq_ref, lambda y, rows, r: put(mq_ref)(moba_rot(y, rows), rows, r)),
        (wmk_ref, put_moba_k),
        (wmv_ref, put(mv_ref)),
        (wga_ref, lambda y, rows, r: put(ga_ref)(jax.nn.sigmoid(y), rows, r)),
        (wgb_ref, lambda y, rows, r: put(gb_ref)(jax.nn.sigmoid(y), rows, r)),
    ]
    steps = []
    for r in range(h_ref.shape[0] // MOBA_BLOCK):
        rows = slice(r * MOBA_BLOCK, (r + 1) * MOBA_BLOCK)
        hb = h_ref[rows, :].astype(BF16)
        steps += [(hb, w_ref, post, rows, r) for w_ref, post in segments]

    def matmul(step):
        return jnp.dot(step[0], step[1][...], preferred_element_type=F32)

    ahead = matmul(steps[0])
    for n, (_, _, post, rows, r) in enumerate(steps):
        y = ahead
        if n + 1 < len(steps):
            ahead = matmul(steps[n + 1])
        post(y, rows, r)


def _rotary_tables(S):
    pos = jnp.arange(S).astype(F32)
    inv = 1.0 / (RET_ROT_BASE ** jnp.linspace(0.0, 1.0, RET_DK // 2, dtype=F32))
    ang = pos[:, None] * inv[None, :]
    cos, sin = jnp.cos(ang), jnp.sin(ang)
    rcos = jnp.concatenate([cos, cos], axis=1)
    rsin = jnp.concatenate([-sin, sin], axis=1)
    log_g = jnp.log(1.0 - 2.0 ** (-5.0 - jnp.arange(RET_HEADS, dtype=F32)))
    idx1 = (jnp.arange(S) % RET_CHUNK).astype(F32) + 1.0
    qdec = jnp.exp(log_g[None, :] * idx1[:, None])
    kdec = jnp.exp(-log_g[None, :] * idx1[:, None]) * (RET_DK ** -0.5)

    def per_head(tab, dec):
        return (tab[:, None, :] * dec[:, :, None]).reshape(S, RET_HEADS * RET_DK)

    qcos, qsin = per_head(rcos, qdec), per_head(rsin, qdec)
    kcos, ksin = per_head(rcos, kdec), per_head(rsin, kdec)
    inv = 1.0 / (ROPE_THETA ** (jnp.arange(0, ROPE_DIMS, 2, dtype=F32) / ROPE_DIMS))
    ang = pos[:, None] * inv[None, :]
    cos, sin = jnp.cos(ang), jnp.sin(ang)
    half = ROPE_DIMS // 2
    rest = MOBA_DH - ROPE_DIMS
    ones = jnp.ones((S, rest), F32)
    zeros = jnp.zeros((S, rest), F32)
    zh = jnp.zeros((S, half), F32)
    reps = LANES // MOBA_DH
    mcos = jnp.tile(jnp.concatenate([cos, cos, ones], axis=1), (1, reps))
    msa = jnp.tile(jnp.concatenate([-sin, zh, zeros], axis=1), (1, reps))
    msb = jnp.tile(jnp.concatenate([zh, sin, zeros], axis=1), (1, reps))
    return qcos, qsin, kcos, ksin, mcos, msa, msb


def _in_proj(h, w_in, S, *, tm):
    T, D = h.shape
    nqk = RET_HEADS * RET_DK
    nv = RET_HEADS * RET_DV
    nm = MOBA_HEADS * MOBA_DH
    assert tm % MOBA_BLOCK == 0 and S % tm == 0
    sizes = (nqk, nqk, nv, nv, nm, nm, nm, D, D)
    offs = [0]
    for s in sizes:
        offs.append(offs[-1] + s)
    assert offs[-1] == w_in.shape[1]
    wb = w_in.astype(BF16)
    ws = [wb[:, offs[i]:offs[i + 1]] for i in range(len(sizes))]
    tabs = _rotary_tables(S)
    spt = S // tm
    ret_tab = pl.BlockSpec((tm, nqk), lambda s, b: (s, 0))
    moba_tab = pl.BlockSpec((tm, LANES), lambda s, b: (s, 0))
    out_dtypes = (BF16, BF16, BF16, BF16, F32, BF16, BF16, BF16, BF16)
    bpt = tm // MOBA_BLOCK
    return pl.pallas_call(
        _in_proj_kernel,
        grid=(spt, T // S),
        in_specs=[pl.BlockSpec((tm, D), lambda s, b: (b * spt + s, 0))]
        + [_resident((D, n)) for n in sizes]
        + [ret_tab] * 4 + [moba_tab] * 3,
        out_specs=[pl.BlockSpec((tm, n), lambda s, b: (b * spt + s, 0)) for n in sizes]
        + [pl.BlockSpec((bpt, 1, nm), lambda s, b: (b * spt + s, 0, 0))],
        out_shape=[jax.ShapeDtypeStruct((T, n), dt) for n, dt in zip(sizes, out_dtypes)]
        + [jax.ShapeDtypeStruct((T // MOBA_BLOCK, 1, nm), F32)],
        compiler_params=_cparams("parallel", "parallel"),
        name="in_proj",
    )(h, *ws, *tabs)


def _retention_kernel(q_ref, k_ref, v_ref, sg_ref, o_ref, state_ref, *, chunks_per_step):
    C = RET_CHUNK

    @pl.when(pl.program_id(1) == 0)
    def _():
        state_ref[...] = jnp.zeros_like(state_ref)

    row = lax.broadcasted_iota(jnp.int32, (C, C), 0)
    col = lax.broadcasted_iota(jnp.int32, (C, C), 1)
    tril = col <= row
    heads = range(RET_HEADS)

    def qkv(c, h):
        rows = slice(c * C, (c + 1) * C)
        kcols = slice(h * RET_DK, (h + 1) * RET_DK)
        vcols = slice(h * RET_DV, (h + 1) * RET_DV)
        return q_ref[0, rows, kcols], k_ref[0, rows, kcols], v_ref[0, rows, vcols]

    states = [[state_ref[h] for h in heads]]
    for c in range(chunks_per_step):
        kvs = []
        for h in heads:
            _, k, v = qkv(c, h)
            kvs.append(lax.dot_general(k, v, TN_DIMS, preferred_element_type=F32))
        states.append([(_ret_decay(h) ** C) * (states[c][h] + kvs[h]) for h in heads])
    for h in heads:
        state_ref[h] = states[chunks_per_step][h]

    for c in range(chunks_per_step):
        rows = slice(c * C, (c + 1) * C)
        qs, ks, vs = zip(*[qkv(c, h) for h in heads])
        scores = [lax.dot_general(qs[h], ks[h], NT_DIMS, preferred_element_type=F32) for h in heads]
        scores = [jnp.where(tril, s, 0.0).astype(BF16) for s in scores]
        outs = [jnp.dot(jnp.concatenate([scores[h], qs[h]], axis=1),
                        jnp.concatenate([vs[h], states[c][h].astype(BF16)], axis=0),
                        preferred_element_type=F32) for h in heads]
        mus = [jnp.mean(o, axis=-1, keepdims=True) for o in outs]
        ocs = [outs[h] - mus[h] for h in heads]
        vrs = [jnp.mean(oc * oc, axis=-1, keepdims=True) for oc in ocs]
        for h in heads:
            vcols = slice(h * RET_DV, (h + 1) * RET_DV)
            y = ocs[h] * lax.rsqrt(vrs[h] + GN_EPS) * sg_ref[0, rows, vcols].astype(F32)
            o_ref[0, rows, vcols] = y.astype(o_ref.dtype)


def _retention(rq, rk, rv, sg, *, ts):
    B, S, _ = rq.shape
    nqk = RET_HEADS * RET_DK
    nv = RET_HEADS * RET_DV
    qk_spec = pl.BlockSpec((1, ts, nqk), lambda b, s: (b, s, 0))
    v_spec = pl.BlockSpec((1, ts, nv), lambda b, s: (b, s, 0))
    return pl.pallas_call(
        functools.partial(_retention_kernel, chunks_per_step=ts // RET_CHUNK),
        grid=(B, S // ts),
        in_specs=[qk_spec, qk_spec, v_spec, v_spec],
        out_specs=v_spec,
        out_shape=jax.ShapeDtypeStruct((B, S, nv), BF16),
        scratch_shapes=[pltpu.VMEM((RET_HEADS, RET_DK, RET_DV), F32)],
        compiler_params=_cparams("parallel", "arbitrary"),
        name="retention",
    )(rq, rk, rv, sg)


def _split_dot(a, b):
    a_hi = a.astype(BF16)
    a_lo = (a - a_hi.astype(F32)).astype(BF16)
    b_hi = b.astype(BF16)
    b_lo = (b - b_hi.astype(F32)).astype(BF16)
    dot = functools.partial(jnp.dot, preferred_element_type=F32)
    return dot(a_hi, b_hi) + (dot(a_hi, b_lo) + dot(a_lo, b_hi))


def _moba_kernel(q_ref, k_ref, v_ref, kmean_ref, o_ref, *, n_blocks):
    BLK = MOBA_BLOCK
    D = MOBA_DH
    S = n_blocks * BLK
    scale = (D ** -0.5) * math.log2(math.e)
    n_heads = LANES // D
    q_t = q_ref[0].T
    k2 = k_ref[0].astype(F32)
    v_t = v_ref[0].astype(F32).T.astype(BF16)
    kmean2 = kmean_ref[0]
    lane = lax.broadcasted_iota(jnp.int32, (S, LANES), 1)
    key_blk = lax.broadcasted_iota(jnp.int32, (S, LANES), 0) // BLK
    key_row = lax.broadcasted_iota(jnp.int32, (BLK, BLK), 0)
    query_col = lax.broadcasted_iota(jnp.int32, (BLK, BLK), 1)
    blk_t = lax.broadcasted_iota(jnp.int32, (n_blocks, S), 0)
    qblk_t = lax.broadcasted_iota(jnp.int32, (n_blocks, S), 1) // BLK
    own_lanes = [(lane[:n_blocks] >= hh * D) & (lane[:n_blocks] < (hh + 1) * D) for hh in range(n_heads)]
    km_all = jnp.concatenate([jnp.where(own_lanes[hh], kmean2, 0.0) for hh in range(n_heads)], axis=0)
    gate_all = _split_dot(km_all, q_t)
    heads = []
    for hh in range(n_heads):
        gate_t = gate_all[hh * n_blocks:(hh + 1) * n_blocks]
        rank = jnp.zeros((n_blocks, S), jnp.int32)
        for jp in range(n_blocks - 1):
            gj = gate_t[jp:jp + 1]
            beats = (gj > gate_t) | ((gj == gate_t) & (jp < blk_t))
            rank = rank + jnp.where(beats & (jp < qblk_t), 1, 0)
        keep = ((rank < MOBA_TOPK) & (blk_t < qblk_t)) | (blk_t == qblk_t)
        sel_t = jnp.where(keep, 0.0, MASKED).astype(F32)
        own = (lane >= hh * D) & (lane < (hh + 1) * D)
        aux0 = D - hh * D
        qd = q_t[hh * D:(hh + 1) * D] * scale
        pad = jnp.zeros((LANES - D - n_blocks, S), F32)
        parts = [qd, sel_t, pad] if hh == 0 else [sel_t, pad, qd]
        q_aug_t = jnp.concatenate(parts, axis=0).astype(BF16)
        onehot = jnp.where(lane - aux0 == key_blk, 1.0, 0.0)
        k_aug = jnp.where(own, k2, onehot).astype(BF16)
        v_aug_t = jnp.concatenate([v_t[hh * D:(hh + 1) * D], jnp.ones((ONES_ROWS, S), BF16)], axis=0)
        heads.append((q_aug_t, k_aug, v_aug_t))

    KS = MOBA_KEY_STEP
    def logits_steps(hh, i, st):
        q_aug_t, k_aug, _ = heads[hh]
        q_i = q_aug_t[:, i * BLK:(i + 1) * BLK]
        st["s"] = []

        def step(t):
            k0 = t * KS
            s = jnp.dot(k_aug[k0:k0 + KS], q_i, preferred_element_type=F32)
            if k0 >= i * BLK:
                s = jnp.where(key_row[:KS] + (k0 - i * BLK) <= query_col[:KS], s, MASKED)
            st["s"].append(s)
            m8 = jnp.max(s.reshape(KS // SUBLANES, SUBLANES, BLK), axis=0)
            st["m8"] = m8 if "m8" not in st else jnp.maximum(st["m8"], m8)

        return [functools.partial(step, t) for t in range((i + 1) * BLK // KS - 1, -1, -1)]

    def weight_steps(hh, i, st):
        v_h = heads[hh][2]
        n_steps = (i + 1) * BLK // KS

        def step(n):
            k0 = (n_steps - 1 - n) * KS
            if n == 0:
                st["m"] = jnp.max(st["m8"], axis=0, keepdims=True)
            p = jnp.exp2(st["s"][n] - st["m"])
            pv = jnp.dot(v_h[:, k0:k0 + KS], p.astype(BF16), preferred_element_type=F32)
            st["acc"] = pv if n == 0 else st["acc"] + pv
            if n == n_steps - 1:
                st["out"] = st["acc"][:D] / st["acc"][D:D + 1]

        return [functools.partial(step, n) for n in range(n_steps)]

    def zipped(lists):
        return [step for group in zip(*lists) for step in group]

    states = [dict() for _ in range(n_blocks * n_heads)]
    pending = []
    for i in range(n_blocks):
        group = [states[i * n_heads + hh] for hh in range(n_heads)]
        fresh = zipped([logits_steps(hh, i, group[hh]) for hh in range(n_heads)])
        while fresh or pending:
            if fresh:
                fresh.pop(0)()
            if pending:
                pending.pop(0)()
        pending = zipped([weight_steps(hh, i, group[hh]) for hh in range(n_heads)])
    for step in pending:
        step()
    out_t = jnp.concatenate(
        [jnp.concatenate([states[i * n_heads + hh]["out"] for i in range(n_blocks)], axis=1) for hh in range(n_heads)],
        axis=0)
    o_ref[0] = out_t.T.astype(o_ref.dtype)


def _moba(mq, mk, mv, kmean):
    B, S, W = mq.shape
    n_blocks = S // MOBA_BLOCK
    spec = pl.BlockSpec((1, S, LANES), lambda b, c: (b, 0, c))
    return pl.pallas_call(
        functools.partial(_moba_kernel, n_blocks=n_blocks),
        grid=(B, W // LANES),
        in_specs=[spec, spec, spec, pl.BlockSpec((1, n_blocks, LANES), lambda b, c: (b, 0, c))],
        out_specs=spec,
        out_shape=jax.ShapeDtypeStruct((B, S, W), BF16),
        compiler_params=_cparams("parallel", "parallel"),
        name="moba",
    )(mq, mk, mv, kmean)


def _mix_ln_kernel(h_ref, yr_ref, ym_ref, ga_ref, gb_ref, pa_ref, pb_ref, wo_ref, g_ref, b_ref, o_ref, *, alpha):
    tm = h_ref.shape[0]
    subs = [slice(r, r + MIX_SUB_ROWS) for r in range(0, tm, MIX_SUB_ROWS)]
    yas = [jnp.dot(yr_ref[rows, :], pa_ref[...], preferred_element_type=F32) for rows in subs]
    ybs = [jnp.dot(ym_ref[rows, :], pb_ref[...], preferred_element_type=F32) for rows in subs]
    mixes = [(ga_ref[rows, :].astype(F32) * ya + gb_ref[rows, :].astype(F32) * yb).astype(BF16)
             for rows, ya, yb in zip(subs, yas, ybs)]
    outs = [jnp.dot(mix, wo_ref[...], preferred_element_type=F32) for mix in mixes]
    for rows, out in zip(subs, outs):
        o_ref[rows, :] = _layer_norm(alpha * h_ref[rows, :] + out, g_ref[...], b_ref[...])


def _mix_ln(h, y_ret, y_moba, ga, gb, ret_proj, moba_proj, w_out, g, b, *, alpha, tm):
    T, D = h.shape
    NR = y_ret.shape[1]
    NM = y_moba.shape[1]

    def tile(n):
        return pl.BlockSpec((tm, n), lambda i: (i, 0))

    return pl.pallas_call(
        functools.partial(_mix_ln_kernel, alpha=alpha),
        grid=(T // tm,),
        in_specs=[tile(D), tile(NR), tile(NM), tile(D), tile(D),
                  _resident((NR, D)), _resident((NM, D)), _resident((D, D)), _resident((1, D)), _resident((1, D))],
        out_specs=tile(D),
        out_shape=jax.ShapeDtypeStruct((T, D), F32),
        compiler_params=_cparams("parallel"),
        name="mix_ln",
    )(h, y_ret, y_moba, ga, gb, ret_proj.astype(BF16), moba_proj.astype(BF16), w_out.astype(BF16),
      g.reshape(1, D), b.reshape(1, D))


def kernel(x, ln1_g, ln1_b, ffn1_w_gu, ffn1_w_down, w_in, ret_proj, moba_proj, w_out, lnm_g, lnm_b,
           ffn2_w_gu, ffn2_w_down, ln2_g, ln2_b):
    B, S, D = x.shape
    depth = ffn1_w_gu.shape[0]
    alpha = (2.0 * depth) ** 0.25
    T = B * S
    h = x.reshape(T, D)
    for l in range(depth):
        h = _ffn_ln(h, ffn1_w_gu[l], ffn1_w_down[l], ln1_g[l], ln1_b[l], alpha=alpha, tm=FFN_TILE_ROWS)
        rq, rk, rv, sg, mq, mk, mv, ga, gb, kmean = _in_proj(h, w_in[l], S, tm=IN_PROJ_TILE_ROWS)
        y_ret = _retention(rq.reshape(B, S, -1), rk.reshape(B, S, -1), rv.reshape(B, S, -1), sg.reshape(B, S, -1),
                           ts=RET_TILE_ROWS)
        y_moba = _moba(mq.reshape(B, S, -1), mk.reshape(B, S, -1), mv.reshape(B, S, -1),
                       kmean.reshape(B, S // MOBA_BLOCK, -1))
        h = _mix_ln(h, y_ret.reshape(T, -1), y_moba.reshape(T, -1), ga, gb, ret_proj[l], moba_proj[l], w_out[l],
                    lnm_g[l], lnm_b[l], alpha=alpha, tm=MIX_TILE_ROWS)
        h = _ffn_ln(h, ffn2_w_gu[l], ffn2_w_down[l], ln2_g[l], ln2_b[l], alpha=alpha, tm=FFN_TILE_ROWS)
    return h.reshape(B, S, D)
```
